```python
import math
import jax, jax.numpy as jnp
from jax import lax
import numpy as np

D_MODEL = 1024
BATCH = 32
SEQ = 256
DEPTH = 4
DEC_BATCH = 2
DEC_SEQ = 4096
PAST_LEN = 512

GRID_W = 64
HEAD_DIM = 64
NORM_EPS = 1e-6
ROPE_BASE = 10000.0
ML_WIDTH = D_MODEL // 4
ML_HEADS = ML_WIDTH // HEAD_DIM
ML_CHUNK = 64
N_GATES = 4 * ML_HEADS
ATT_WIDTH = D_MODEL // 2
ATT_HEADS = ATT_WIDTH // HEAD_DIM
ATT_KV_HEADS = ATT_HEADS // 4
GQA_GROUP = ATT_HEADS // ATT_KV_HEADS
KV_WIDTH = ATT_KV_HEADS * HEAD_DIM
WINDOW = 128
ATT_BLOCK = 128
S5_WIDTH = D_MODEL // 4
S5_GROUP_SIZE = 16
S5_GROUPS = S5_WIDTH // S5_GROUP_SIZE
S5_STATE = 64
D_MIX = ML_WIDTH + ATT_WIDTH + S5_WIDTH
D_IN = 4 * ML_WIDTH + N_GATES + ATT_WIDTH + 2 * KV_WIDTH + S5_WIDTH
D_FF = 2816
N_EXPERTS = 8
TOP_K = 2
N_DENSE = (DEPTH + 1) // 2
N_MOE = DEPTH // 2

kernel_name = 'hybrid_mlstm_swa_s5_diffusion_step'


def rmsnorm(x, w):
    xf = x.astype(jnp.float32)
    y = xf * lax.rsqrt(jnp.mean(xf * xf, axis=-1, keepdims=True) + NORM_EPS)
    return (y * w.astype(jnp.float32)).astype(x.dtype)


def project_inputs(h, w_in_l, gate_b_l):
    B, T, _ = h.shape
    z = jnp.einsum('btd,de->bte', h, w_in_l).astype(jnp.float32)
    sizes = (ML_WIDTH, ML_WIDTH, ML_WIDTH, ML_WIDTH, N_GATES, ATT_WIDTH, KV_WIDTH, KV_WIDTH, S5_WIDTH)
    cuts = [int(s) for s in np.cumsum(sizes)[:-1]]
    qm, km, vm, om, gates, qa, ka, va, u = jnp.split(z, cuts, axis=-1)
    heads = lambda a: a.reshape(B, T, -1, HEAD_DIM)
    gates = (gates + gate_b_l.astype(jnp.float32)).reshape(B, T, 4, ML_HEADS)
    return (heads(qm), heads(km) * HEAD_DIM ** -0.5, heads(vm), om, gates,
            heads(qa), heads(ka), heads(va), u.reshape(B, T, S5_GROUPS, S5_GROUP_SIZE))


def mlstm_chunkwise(q, k, v, ig, fg, c0, n0, m0):
    B, T, H, D = q.shape
    L = ML_CHUNK
    nc = T // L
    to_chunks = lambda a: jnp.moveaxis(a.reshape((B, nc, L) + a.shape[2:]), 1, 0)
    causal = jnp.tril(jnp.ones((L, L), dtype=bool))[None, :, :, None]

    def step(carry, xs):
        C, n, m = carry
        qb, kb, vb, ib, fb = xs
        b = jnp.cumsum(jax.nn.log_sigmoid(fb), axis=1)
        dmat = b[:, :, None, :] - b[:, None, :, :] + ib[:, None, :, :]
        dmat = jnp.where(causal, dmat, -jnp.inf)
        inter = b + m[:, None, :]
        m_t = jnp.maximum(inter, jnp.max(dmat, axis=2))
        w_intra = jnp.exp(dmat - m_t[:, :, None, :])
        w_inter = jnp.exp(inter - m_t)
        qk = jnp.einsum('bthd,bshd->btsh', qb, kb) * w_intra
        num = jnp.einsum('btsh,bshd->bthd', qk, vb) + w_inter[..., None] * jnp.einsum('bthd,bhde->bthe', qb, C)
        den = jnp.sum(qk, axis=2) + w_inter * jnp.einsum('bthd,bhd->bth', qb, n)
        h = num / jnp.maximum(jnp.abs(den), jnp.exp(-m_t))[..., None]
        b_last = b[:, -1]
        d_last = b_last[:, None, :] - b + ib
        m_new = jnp.maximum(b_last + m, jnp.max(d_last, axis=1))
        w_last = jnp.exp(d_last - m_new[:, None, :])
        decay = jnp.exp(b_last + m - m_new)
        C_new = decay[..., None, None] * C + jnp.einsum('bsh,bshd,bshe->bhde', w_last, kb, vb)
        n_new = decay[..., None] * n + jnp.einsum('bsh,bshd->bhd', w_last, kb)
        return (C_new, n_new, m_new), h

    (C, n, m), hs = lax.scan(step, (c0, n0, m0), tuple(map(to_chunks, (q, k, v, ig, fg))))
    return jnp.moveaxis(hs, 0, 1).reshape(B, T, H, D), C, n, m


def mlstm_bidir(q, k, v, o, gates, norm_w, c0, n0, m0):
    f32 = jnp.float32
    B, T = q.shape[:2]
    flip = lambda a: jnp.flip(a, axis=1)
    c0, n0, m0 = c0.astype(f32), n0.astype(f32), m0.astype(f32)
    h_f, c_f, n_f, m_f = mlstm_chunkwise(q, k, v, gates[:, :, 0], gates[:, :, 1], c0[:, 0], n0[:, 0], m0[:, 0])
    h_b, c_b, n_b, m_b = mlstm_chunkwise(flip(q), flip(k), flip(v), flip(gates[:, :, 2]), flip(gates[:, :, 3]),
                                         c0[:, 1], n0[:, 1], m0[:, 1])
    h = h_f + flip(h_b)
    h = h * lax.rsqrt(jnp.mean(h * h, axis=-1, keepdims=True) + NORM_EPS) * norm_w.astype(f32).reshape(ML_HEADS, HEAD_DIM)
    out = jax.nn.sigmoid(o) * h.reshape(B, T, ML_WIDTH)
    return out, jnp.stack([c_f, c_b], axis=1), jnp.stack([n_f, n_b], axis=1), jnp.stack([m_f, m_b], axis=1)


def rope_2d(x):
    B, T, H, D = x.shape
    rows = T // GRID_W
    row = jnp.repeat(jnp.arange(rows, dtype=jnp.float32), GRID_W)
    col = jnp.tile(jnp.arange(GRID_W, dtype=jnp.float32), rows)
    half = D // 2
    inv = 1.0 / (ROPE_BASE ** (jnp.arange(0, half, 2, dtype=jnp.float32) / half))

    def rot(xa, pos):
        ang = pos[:, None] * inv[None, :]
        ang = jnp.concatenate([ang, ang], axis=-1)[None, :, None, :]
        x1, x2 = xa[..., :half // 2], xa[..., half // 2:]
        return xa * jnp.cos(ang) + jnp.concatenate([-x2, x1], axis=-1) * jnp.sin(ang)

    return jnp.concatenate([rot(x[..., :half], row), rot(x[..., half:], col)], axis=-1)


def sink_softmax(s, sink):
    m = jnp.maximum(jnp.max(s, axis=-1, keepdims=True), sink)
    p = jnp.exp(s - m)
    return p / (jnp.sum(p, axis=-1, keepdims=True) + jnp.exp(sink - m))


def attention_context(q, k, v, sink):
    B, T = q.shape[:2]
    nb = T // ATT_BLOCK
    qb = jnp.moveaxis(q.reshape(B, nb, ATT_BLOCK, ATT_KV_HEADS, GQA_GROUP, HEAD_DIM), 1, 0)
    sink_b = sink.astype(jnp.float32).reshape(1, ATT_KV_HEADS, GQA_GROUP, 1, 1)
    scale = HEAD_DIM ** -0.5

    def one_block(qblk):
        s = jnp.einsum('bqhgd,bshd->bhgqs', qblk, k) * scale
        return jnp.einsum('bhgqs,bshd->bqhgd', sink_softmax(s, sink_b), v)

    o = lax.map(one_block, qb)
    return jnp.moveaxis(o, 0, 1).reshape(B, T, ATT_WIDTH)


def attention_latent(q, k, v, k_ctx, v_ctx, sink):
    B, T = q.shape[:2]
    nb = T // ATT_BLOCK
    k_ctx = k_ctx.astype(jnp.float32)
    v_ctx = v_ctx.astype(jnp.float32)
    qb = q.reshape(B, nb, ATT_BLOCK, ATT_KV_HEADS, GQA_GROUP, HEAD_DIM)
    pad = ((0, 0), (ATT_BLOCK, ATT_BLOCK), (0, 0), (0, 0))

    def band(a):
        ap = jnp.pad(a, pad).reshape(B, nb + 2, ATT_BLOCK, ATT_KV_HEADS, HEAD_DIM)
        return jnp.concatenate([ap[:, :-2], ap[:, 1:-1], ap[:, 2:]], axis=2)

    kw, vw = band(k), band(v)
    qpos = jnp.arange(T).reshape(nb, ATT_BLOCK)
    kpos = jnp.arange(nb)[:, None] * ATT_BLOCK - ATT_BLOCK + jnp.arange(3 * ATT_BLOCK)[None, :]
    valid = (kpos[:, None, :] >= 0) & (kpos[:, None, :] < T) & (jnp.abs(qpos[:, :, None] - kpos[:, None, :]) <= WINDOW)
    scale = HEAD_DIM ** -0.5
    s_loc = jnp.einsum('bnqhgd,bnshd->bhgnqs', qb, kw) * scale
    s_loc = jnp.where(valid[None, None, None], s_loc, -jnp.inf)
    s_ctx = jnp.einsum('bnqhgd,bshd->bhgnqs', qb, k_ctx) * scale
    sink_b = sink.astype(jnp.float32).reshape(1, ATT_KV_HEADS, GQA_GROUP, 1, 1, 1)
    p = sink_softmax(jnp.concatenate([s_loc, s_ctx], axis=-1), sink_b)
    o = (jnp.einsum('bhgnqs,bnshd->bnqhgd', p[..., :3 * ATT_BLOCK], vw)
         + jnp.einsum('bhgnqs,bshd->bnqhgd', p[..., 3 * ATT_BLOCK:], v_ctx))
    return o.reshape(B, T, ATT_WIDTH)


def linear_recurrence_combine(e1, e2):
    a1, b1 = e1
    a2, b2 = e2
    return a1 * a2, a2 * b1 + b2


def s5_bidir(u, lam_re, lam_im, log_dt, b_re, b_im, c_re, c_im, d, glu_w, glu_b, x0_re, x0_im):
    f32 = jnp.float32
    B, T = u.shape[:2]
    cplx = lambda a, b: lax.complex(a.astype(f32), b.astype(f32))
    lam = cplx(lam_re, lam_im)
    lam_bar = jnp.exp(lam * jnp.exp(log_dt.astype(f32))[..., None])
    b_bar = ((lam_bar - 1.0) / lam)[..., None] * cplx(b_re, b_im)[None]
    cmat = cplx(c_re, c_im)
    x0 = cplx(x0_re, x0_im)
    uc = lax.complex(u, jnp.zeros_like(u))

    def run(u_dir, lb, bb, x_init):
        bu = jnp.einsum('gnp,btgp->btgn', bb, u_dir)
        bu = bu.at[:, 0].add(lb * x_init)
        a = jnp.broadcast_to(lb, bu.shape)
        return lax.associative_scan(linear_recurrence_combine, (a, bu), axis=1)[1]

    xs_f = run(uc, lam_bar[0], b_bar[0], x0[:, 0])
    xs_b = run(jnp.flip(uc, axis=1), lam_bar[1], b_bar[1], x0[:, 1])
    y = jnp.real(jnp.einsum('gpn,btgn->btgp', cmat, xs_f + jnp.flip(xs_b, axis=1)))
    y = y + d.astype(f32).reshape(S5_GROUPS, S5_GROUP_SIZE) * u
    z = jax.nn.gelu(y, approximate=False).reshape(B, T, S5_WIDTH)
    out = z * jax.nn.sigmoid(z @ glu_w.astype(f32) + glu_b.astype(f32))
    x_last = jnp.stack([xs_f[:, -1], xs_b[:, -1]], axis=1)
    return out, jnp.real(x_last), jnp.imag(x_last)


def swiglu(h, w_gate, w_up, w_down):
    return (jax.nn.silu(h @ w_gate) * (h @ w_up)) @ w_down


def moe_swiglu(h, router, w_gate, w_up, w_down):
    logits = jnp.einsum('btd,de->bte', h, router).astype(jnp.float32)
    top_v, top_i = lax.top_k(logits, TOP_K)
    top_w = jax.nn.softmax(top_v, axis=-1)
    gates = jnp.sum(jax.nn.one_hot(top_i, N_EXPERTS, dtype=jnp.float32) * top_w[..., None], axis=-2)
    y = jnp.zeros_like(h)
    for e in range(N_EXPERTS):
        y = y + gates[..., e:e + 1].astype(h.dtype) * swiglu(h, w_gate[e], w_up[e], w_down[e])
    return y


def setup_inputs(seed: int = 0) -> dict:
    key = jax.random.key(seed)
    ks = iter(jax.random.split(key, 64))
    f32 = jnp.float32
    nrm = lambda shape, scale: jax.random.normal(next(ks), shape, f32) * scale
    gain = lambda shape: 1.0 + nrm(shape, 0.05)
    fbias = jnp.linspace(3.0, 6.0, ML_HEADS, dtype=f32)
    zh = jnp.zeros((ML_HEADS,), f32)
    gate_base = jnp.concatenate([zh, fbias, zh, fbias])
    n_idx = jnp.arange(S5_STATE, dtype=f32)
    log_dt = jax.random.uniform(next(ks), (DEPTH, 2, S5_GROUPS), f32, math.log(1e-3), math.log(1e-1))
    return {
        'x_prompt': nrm((BATCH, SEQ, D_MODEL), 1.0),
        'x_sample': nrm((DEC_BATCH, DEC_SEQ, D_MODEL), 1.0),
        'c': nrm((DEC_BATCH, D_MODEL), 1.0),
        'cache_k': nrm((DEC_BATCH, DEPTH, PAST_LEN, ATT_KV_HEADS, HEAD_DIM), 1.0),
        'cache_v': nrm((DEC_BATCH, DEPTH, PAST_LEN, ATT_KV_HEADS, HEAD_DIM), 1.0),
        'state_mlstm_c': nrm((DEC_BATCH, DEPTH, 2, ML_HEADS, HEAD_DIM, HEAD_DIM), 0.1),
        'state_mlstm_n': nrm((DEC_BATCH, DEPTH, 2, ML_HEADS, HEAD_DIM), 0.1),
        'state_mlstm_m': nrm((DEC_BATCH, DEPTH, 2, ML_HEADS), 1.0),
        'state_s5_re': nrm((DEC_BATCH, DEPTH, 2, S5_GROUPS, S5_STATE), 0.3),
        'state_s5_im': nrm((DEC_BATCH, DEPTH, 2, S5_GROUPS, S5_STATE), 0.3),
        'c_ctx': nrm((D_MODEL,), 1.0),
        'w_mod': nrm((DEPTH, D_MODEL, 6 * D_MODEL), 0.5 * D_MODEL ** -0.5),
        'b_mod': nrm((DEPTH, 6 * D_MODEL), 0.02),
        'norm_mix_w': gain((DEPTH, D_MODEL)),
        'norm_ffn_w': gain((DEPTH, D_MODEL)),
        'norm_f_w': gain((D_MODEL,)),
        'w_in': nrm((DEPTH, D_MODEL, D_IN), D_MODEL ** -0.5),
        'w_out': nrm((DEPTH, D_MIX, D_MODEL), D_MIX ** -0.5),
        'mlstm_gate_b': gate_base[None, :] + nrm((DEPTH, N_GATES), 0.1),
        'mlstm_norm_w': gain((DEPTH, ML_WIDTH)),
        'attn_sink': nrm((DEPTH, ATT_HEADS), 0.5),
        's5_lam_re': -0.5 + nrm((DEPTH, 2, S5_GROUPS, S5_STATE), 0.01),
        's5_lam_im': math.pi * n_idx + nrm((DEPTH, 2, S5_GROUPS, S5_STATE), 0.01),
        's5_log_dt': log_dt,
        's5_b_re': nrm((DEPTH, S5_GROUPS, S5_STATE, S5_GROUP_SIZE), S5_GROUP_SIZE ** -0.5),
        's5_b_im': nrm((DEPTH, S5_GROUPS, S5_STATE, S5_GROUP_SIZE), S5_GROUP_SIZE ** -0.5),
        's5_c_re': nrm((DEPTH, S5_GROUPS, S5_GROUP_SIZE, S5_STATE), S5_STATE ** -0.5),
        's5_c_im': nrm((DEPTH, S5_GROUPS, S5_GROUP_SIZE, S5_STATE), S5_STATE ** -0.5),
        's5_d': nrm((DEPTH, S5_WIDTH), 1.0),
        's5_glu_w': nrm((DEPTH, S5_WIDTH, S5_WIDTH), S5_WIDTH ** -0.5),
        's5_glu_b': nrm((DEPTH, S5_WIDTH), 0.02),
        'ffn_w_gate': nrm((N_DENSE, D_MODEL, D_FF), D_MODEL ** -0.5),
        'ffn_w_up': nrm((N_DENSE, D_MODEL, D_FF), D_MODEL ** -0.5),
        'ffn_w_down': nrm((N_DENSE, D_FF, D_MODEL), D_FF ** -0.5),
        'moe_router': nrm((N_MOE, D_MODEL, N_EXPERTS), D_MODEL ** -0.5),
        'moe_w_gate': nrm((N_MOE, N_EXPERTS, D_MODEL, D_FF), D_MODEL ** -0.5),
        'moe_w_up': nrm((N_MOE, N_EXPERTS, D_MODEL, D_FF), D_MODEL ** -0.5),
        'moe_w_down': nrm((N_MOE, N_EXPERTS, D_FF, D_MODEL), D_FF ** -0.5),
    }


def reference(x_prompt, x_sample, c, cache_k, cache_v, state_mlstm_c, state_mlstm_n, state_mlstm_m,
              state_s5_re, state_s5_im, c_ctx, w_mod, b_mod, norm_mix_w, norm_ffn_w, norm_f_w, w_in, w_out,
              mlstm_gate_b, mlstm_norm_w, attn_sink, s5_lam_re, s5_lam_im, s5_log_dt, s5_b_re, s5_b_im,
              s5_c_re, s5_c_im, s5_d, s5_glu_w, s5_glu_b, ffn_w_gate, ffn_w_up, ffn_w_down,
              moe_router, moe_w_gate, moe_w_up, moe_w_down):

    def s5_params(l):
        return (s5_lam_re[l], s5_lam_im[l], s5_log_dt[l], s5_b_re[l], s5_b_im[l], s5_c_re[l], s5_c_im[l],
                s5_d[l], s5_glu_w[l], s5_glu_b[l])

    def merge(h, hm, ha, hs, l):
        mix = jnp.concatenate([hm, ha, hs], axis=-1).astype(h.dtype)
        return jnp.einsum('bte,ed->btd', mix, w_out[l])

    def mixer_context(h, l):
        B, T = h.shape[:2]
        f32 = jnp.float32
        qm, km, vm, om, gates, qa, ka, va, u = project_inputs(h, w_in[l], mlstm_gate_b[l])
        hm, mc, mn, mm = mlstm_bidir(qm, km, vm, om, gates, mlstm_norm_w[l],
                                     jnp.zeros((B, 2, ML_HEADS, HEAD_DIM, HEAD_DIM), f32),
                                     jnp.zeros((B, 2, ML_HEADS, HEAD_DIM), f32),
                                     jnp.zeros((B, 2, ML_HEADS), f32))
        ha = attention_context(qa, ka, va, attn_sink[l])
        zs = jnp.zeros((B, 2, S5_GROUPS, S5_STATE), f32)
        hs, sre, sim = s5_bidir(u, *s5_params(l), zs, zs)
        dt = h.dtype
        return merge(h, hm, ha, hs, l), (ka.astype(dt), va.astype(dt), mc.astype(dt), mn.astype(dt),
                                         mm.astype(dt), sre.astype(dt), sim.astype(dt))

    def mixer_latent(h, l):
        qm, km, vm, om, gates, qa, ka, va, u = project_inputs(h, w_in[l], mlstm_gate_b[l])
        hm, _, _, _ = mlstm_bidir(qm, km, vm, om, gates, mlstm_norm_w[l],
                                  state_mlstm_c[:, l], state_mlstm_n[:, l], state_mlstm_m[:, l])
        ha = attention_latent(rope_2d(qa), rope_2d(ka), va, cache_k[:, l], cache_v[:, l], attn_sink[l])
        hs, _, _ = s5_bidir(u, *s5_params(l), state_s5_re[:, l], state_s5_im[:, l])
        return merge(h, hm, ha, hs, l), ()

    def layer(x, cvec, l, mixer):
        mod = jnp.einsum('bd,de->be', jax.nn.silu(cvec), w_mod[l]) + b_mod[l]
        sh_a, sc_a, g_a, sh_f, sc_f, g_f = [m[:, None, :].astype(x.dtype) for m in jnp.split(mod, 6, axis=-1)]
        h = rmsnorm(x, norm_mix_w[l]) * (1 + sc_a) + sh_a
        mix, st = mixer(h, l)
        x = x + g_a * mix.astype(x.dtype)
        h = rmsnorm(x, norm_ffn_w[l]) * (1 + sc_f) + sh_f
        i = l // 2
        if l % 2 == 0:
            f = swiglu(h, ffn_w_gate[i], ffn_w_up[i], ffn_w_down[i])
        else:
            f = moe_swiglu(h, moe_router[i], moe_w_gate[i], moe_w_up[i], moe_w_down[i])
        return x + g_f * f.astype(x.dtype), st

    x = x_prompt
    per_layer = [[] for _ in range(7)]
    for l in range(DEPTH):
        x, st = layer(x, c_ctx[None, :], l, mixer_context)
        for lst, a in zip(per_layer, st):
            lst.append(a)
    y_prompt = rmsnorm(x, norm_f_w)
    new_cache_k = jnp.stack(per_layer[0], axis=1)
    new_cache_v = jnp.stack(per_layer[1], axis=1)
    new_state_mlstm_c = jnp.stack(per_layer[2], axis=1)
    new_state_mlstm_n = jnp.stack(per_layer[3], axis=1)
    new_state_mlstm_m = jnp.stack(per_layer[4], axis=1)
    new_state_s5_re = jnp.stack(per_layer[5], axis=1)
    new_state_s5_im = jnp.stack(per_layer[6], axis=1)

    x = x_sample
    for l in range(DEPTH):
        x, _ = layer(x, c, l, mixer_latent)
    y_sample = rmsnorm(x, norm_f_w)

    return (y_prompt, y_sample, new_cache_k, new_cache_v, new_state_mlstm_c, new_state_mlstm_n,
            new_state_mlstm_m, new_state_s5_re, new_state_s5_im)
```

```python
import functools
import math

import jax
import jax.numpy as jnp
import numpy as np
from jax import lax
from jax.experimental import pallas as pl
from jax.experimental.pallas import tpu as pltpu

F32 = jnp.float32
BF16 = jnp.bfloat16
HI = lax.Precision.HIGHEST

D_MODEL = 1024
BATCH = 32
SEQ = 256
DEPTH = 4
DEC_BATCH = 2
DEC_SEQ = 4096
PAST_LEN = 512
GRID_W = 64
HEAD_DIM = 64
NORM_EPS = 1e-6
ROPE_BASE = 10000.0
ML_WIDTH = 256
ML_HEADS = 4
ML_CHUNK = 64
N_GATES = 16
ATT_WIDTH = 512
ATT_HEADS = 8
ATT_KV_HEADS = 2
GQA_GROUP = 4
KV_WIDTH = 128
WINDOW = 128
ATT_BLOCK = 128
S5_WIDTH = 256
S5_GROUP_SIZE = 16
S5_GROUPS = 16
S5_STATE = 64
S5_LANES = S5_GROUPS * S5_STATE
D_FF = 2816
N_EXPERTS = 8

N_CTX = BATCH * SEQ
N_LAT = DEC_BATCH * DEC_SEQ
N_TOK = N_CTX + N_LAT
N_MOD_ROWS = 8

V7X_VMEM_LIMIT = 56 * 1024 * 1024
NEG_BIG = -1e30

TM = 512
TM_FFN = 1024
TF = 256
S5_TB_ROWS = 512

NT_DIMS = (((1,), (1,)), ((), ()))
TN_DIMS = (((0,), (0,)), ((), ()))


def _cparams(sem=None, vmem=V7X_VMEM_LIMIT):
    return pltpu.CompilerParams(dimension_semantics=sem, vmem_limit_bytes=vmem)


def _mod_row(i, tm):
    nctx = N_CTX // tm
    return jnp.where(i < nctx, 0, 1 + (i - nctx) // (DEC_SEQ // tm))


def _rmsnorm(x, w):
    return x * lax.rsqrt(jnp.mean(x * x, axis=-1, keepdims=True) + NORM_EPS) * w


def _mod_kernel(c_ref, w_ref, b_ref, o_ref):
    c = c_ref[...]
    s = c * jax.nn.sigmoid(c)
    o_ref[0] = jnp.dot(s, w_ref[0], precision=HI, preferred_element_type=F32) + b_ref[0]


def _modulation(cvecs, w_mod, b_mod):
    tn = 1536
    return pl.pallas_call(
        _mod_kernel,
        grid=(DEPTH, 6 * D_MODEL // tn),
        in_specs=[
            pl.BlockSpec((N_MOD_ROWS, D_MODEL), lambda l, j: (0, 0)),
            pl.BlockSpec((1, D_MODEL, tn), lambda l, j: (l, 0, j)),
            pl.BlockSpec((1, 1, tn), lambda l, j: (l, 0, j)),
        ],
        out_specs=pl.BlockSpec((1, N_MOD_ROWS, tn), lambda l, j: (l, 0, j)),
        out_shape=jax.ShapeDtypeStruct((DEPTH, N_MOD_ROWS, 6 * D_MODEL), F32),
        compiler_params=_cparams(("arbitrary", "arbitrary")),
        name="modulation",
    )(cvecs, w_mod, b_mod.reshape(DEPTH, 1, 6 * D_MODEL))


def _proj_kernel(x_ref, mod_ref, nw_ref, wmain_ref, wgt_ref, gb_ref, cos_ref, sin_ref,
                 qkvm_ref, om_ref, g_ref, qa_ref, ka_ref, va_ref, kv_ref, u_ref):
    m = mod_ref[0]
    h = _rmsnorm(x_ref[...], nw_ref[...]) * (1.0 + m[1:2]) + m[0:1]
    z = jnp.dot(h.astype(BF16), wmain_ref[...], preferred_element_type=F32)
    qkvm_ref[...] = z[:, 0:768].astype(BF16)
    om_ref[...] = z[:, 768:1024]
    kv_ref[...] = z[:, 1536:1792]
    va_ref[...] = z[:, 1664:1792].astype(BF16)
    u_ref[...] = z[:, 1792:2048]

    cos = cos_ref[...]
    sin = sin_ref[...]
    lane = lax.broadcasted_iota(jnp.int32, cos.shape, 1)
    first = (lane % 32) < 16

    def rope(xs):
        rot = jnp.where(first, pltpu.roll(xs, 112, 1), pltpu.roll(xs, 16, 1))
        return xs * cos + rot * sin

    ka_ref[...] = rope(z[:, 1536:1664]).astype(BF16)
    for j in range(ATT_WIDTH // 128):
        qa_ref[:, 128 * j:128 * (j + 1)] = rope(z[:, 1024 + 128 * j:1024 + 128 * (j + 1)]).astype(BF16)

    g = lax.dot_general(wgt_ref[...], h, NT_DIMS, precision=HI, preferred_element_type=F32) + gb_ref[...]
    for c in range(g.shape[1] // ML_CHUNK):
        g_ref[c] = g[:, ML_CHUNK * c:ML_CHUNK * (c + 1)]


def _project(x, mod_l, norm_w, w_main, w_gt, gate_b, cos_tab, sin_tab):
    nblk = N_TOK // TM
    nctx = N_CTX // TM
    per_seq = DEC_SEQ // TM

    def tab_idx(i):
        return jnp.where(i < nctx, 0, 1 + (i - nctx) % per_seq)

    row = lambda i: (i, 0)
    out_shapes = (
        jax.ShapeDtypeStruct((N_TOK, 768), BF16),
        jax.ShapeDtypeStruct((N_TOK, ML_WIDTH), F32),
        jax.ShapeDtypeStruct((N_TOK // ML_CHUNK, N_GATES, ML_CHUNK), F32),
        jax.ShapeDtypeStruct((N_TOK, ATT_WIDTH), BF16),
        jax.ShapeDtypeStruct((N_TOK, KV_WIDTH), BF16),
        jax.ShapeDtypeStruct((N_TOK, KV_WIDTH), BF16),
        jax.ShapeDtypeStruct((N_TOK, 2 * KV_WIDTH), F32),
        jax.ShapeDtypeStruct((N_TOK, S5_WIDTH), F32),
    )
    return pl.pallas_call(
        _proj_kernel,
        grid=(nblk,),
        in_specs=[
            pl.BlockSpec((TM, D_MODEL), row),
            pl.BlockSpec((1, 6, D_MODEL), lambda i: (_mod_row(i, TM), 0, 0)),
            pl.BlockSpec((1, D_MODEL), lambda i: (0, 0)),
            pl.BlockSpec((D_MODEL, 2048), lambda i: (0, 0)),
            pl.BlockSpec((N_GATES, D_MODEL), lambda i: (0, 0)),
            pl.BlockSpec((N_GATES, 1), lambda i: (0, 0)),
            pl.BlockSpec((TM, 128), lambda i: (tab_idx(i), 0)),
            pl.BlockSpec((TM, 128), lambda i: (tab_idx(i), 0)),
        ],
        out_specs=[
            pl.BlockSpec((TM, 768), row),
            pl.BlockSpec((TM, ML_WIDTH), row),
            pl.BlockSpec((TM // ML_CHUNK, N_GATES, ML_CHUNK), lambda i: (i, 0, 0)),
            pl.BlockSpec((TM, ATT_WIDTH), row),
            pl.BlockSpec((TM, KV_WIDTH), row),
            pl.BlockSpec((TM, KV_WIDTH), row),
            pl.BlockSpec((TM, 2 * KV_WIDTH), row),
            pl.BlockSpec((TM, S5_WIDTH), row),
        ],
        out_shape=out_shapes,
        compiler_params=_cparams(("arbitrary",)),
        name="project",
    )(x, mod_l, norm_w, w_main, w_gt, gate_b, cos_tab, sin_tab)


def _rope_tables():
    t = np.arange(DEC_SEQ)
    rowp = (t // GRID_W).astype(np.float32)
    colp = (t % GRID_W).astype(np.float32)
    half = HEAD_DIM // 2
    inv = (1.0 / (ROPE_BASE ** (np.arange(0, half, 2, dtype=np.float32) / half))).astype(np.float32)
    lane = np.arange(128)
    d = lane % HEAD_DIM
    freq = inv[(d % half) % (half // 2)]
    pos = np.where(d[None, :] < half, rowp[:, None], colp[:, None]).astype(np.float32)
    ang = jnp.asarray(pos) * jnp.asarray(freq)[None, :]
    sign = np.where((d % half) < half // 2, -1.0, 1.0).astype(np.float32)
    cos = jnp.concatenate([jnp.ones((TM, 128), F32), jnp.cos(ang)], axis=0)
    sin = jnp.concatenate([jnp.zeros((TM, 128), F32), jnp.sin(ang) * sign[None, :]], axis=0)
    return cos, sin


def _mlstm_kernel(qkv_ref, om_ref, g_ref, c0_ref, n0_ref, m0_ref, nw_ref,
                  out_ref, c_ref, n_ref, m_ref, hf_ref, hb_ref, *, seq_len):
    L = ML_CHUNK
    nc = seq_len // L
    c_ref[...] = c0_ref[...]
    n_ref[...] = n0_ref[...]
    m_ref[...] = m0_ref[...]

    r_i = lax.broadcasted_iota(jnp.int32, (L, L), 0)
    c_i = lax.broadcasted_iota(jnp.int32, (L, L), 1)
    eye = r_i == c_i
    le = (r_i <= c_i).astype(F32)
    ge = (r_i >= c_i).astype(F32)

    def body(j, carry):
        for d in range(2):
            c = j if d == 0 else nc - 1 - j
            r0 = pl.multiple_of(c * L, L)
            qkv = qkv_ref[pl.ds(r0, L), :]
            g = g_ref[c]
            lsg = jax.nn.log_sigmoid(g)
            cum_rc = le if d == 0 else ge
            cum_tr = ge if d == 0 else le
            b_rows = jnp.dot(lsg, cum_rc, precision=HI, preferred_element_type=F32)
            b_cols = lax.dot_general(cum_tr, lsg, NT_DIMS, precision=HI, preferred_element_type=F32)
            causal = (r_i >= c_i) if d == 0 else (r_i <= c_i)
            last = L - 1 if d == 0 else 0
            h_dst = hf_ref if d == 0 else hb_ref
            for hd in range(ML_HEADS):
                gi = 8 * d + hd
                gf = 8 * d + 4 + hd
                sidx = 4 * d + hd
                q = qkv[:, 64 * hd:64 * (hd + 1)]
                k = qkv[:, 256 + 64 * hd:256 + 64 * (hd + 1)]
                v = qkv[:, 512 + 64 * hd:512 + 64 * (hd + 1)]
                bc = b_cols[:, gf:gf + 1]
                br = b_rows[gf:gf + 1, :]
                ir = g[gi:gi + 1, :]
                dmat = jnp.where(causal, bc - br + ir, NEG_BIG)
                m_prev = m_ref[0, sidx:sidx + 1, 0:1]
                inter = bc + m_prev
                m_t = jnp.maximum(inter, jnp.max(dmat, axis=1, keepdims=True))
                w_intra = jnp.exp(dmat - m_t)
                w_inter = jnp.exp(inter - m_t)
                qk = lax.dot_general(q, k, NT_DIMS, preferred_element_type=F32) * w_intra
                c_st = c_ref[0, d, hd]
                n_st = n_ref[0, sidx:sidx + 1, :]
                num = (jnp.dot(qk.astype(BF16), v, preferred_element_type=F32)
                       + w_inter * jnp.dot(q, c_st.astype(BF16), preferred_element_type=F32))
                qf = q.astype(F32)
                den = (jnp.sum(qk, axis=1, keepdims=True)
                       + w_inter * jnp.sum(qf * n_st, axis=1, keepdims=True))
                hh = num / jnp.maximum(jnp.abs(den), jnp.exp(-m_t))
                h_dst[pl.ds(r0, L), 64 * hd:64 * (hd + 1)] = hh

                d_last = dmat[last:last + 1, :]
                b_last = bc[last:last + 1, :]
                m_new = jnp.maximum(b_last + m_prev, jnp.max(d_last, axis=1, keepdims=True))
                w_last = jnp.exp(d_last - m_new)
                decay = jnp.exp(b_last + m_prev - m_new)
                w_col = jnp.sum(jnp.where(eye, w_last, 0.0), axis=1, keepdims=True)
                kw = k.astype(F32) * w_col
                c_ref[0, d, hd] = decay * c_st + lax.dot_general(
                    kw.astype(BF16), v, TN_DIMS, preferred_element_type=F32)
                n_ref[0, sidx:sidx + 1, :] = decay * n_st + jnp.sum(kw, axis=0, keepdims=True)
                m_ref[0, sidx:sidx + 1, :] = jnp.broadcast_to(m_new, (1, 128))
        return carry

    lax.fori_loop(0, nc, body, 0)

    hr = lax.broadcasted_iota(jnp.int32, (ML_WIDTH, ML_WIDTH), 0) // HEAD_DIM
    hc = lax.broadcasted_iota(jnp.int32, (ML_WIDTH, ML_WIDTH), 1) // HEAD_DIM
    head_mean = jnp.where(hr == hc, 1.0 / HEAD_DIM, 0.0).astype(F32)
    rb = 256

    def fin(i, carry):
        r0 = pl.multiple_of(i * rb, rb)
        h = hf_ref[pl.ds(r0, rb), :] + hb_ref[pl.ds(r0, rb), :]
        ms = jnp.dot(h * h, head_mean, precision=HI, preferred_element_type=F32)
        hn = h * lax.rsqrt(ms + NORM_EPS) * nw_ref[...]
        out_ref[pl.ds(r0, rb), :] = (jax.nn.sigmoid(om_ref[pl.ds(r0, rb), :]) * hn).astype(BF16)
        return carry

    lax.fori_loop(0, seq_len // rb, fin, 0)


def _mlstm(qkvm, om, gates, c0, n0, m0, norm_w, *, seq_len, n_seq, blk0):
    nchunk = seq_len // ML_CHUNK
    kern = functools.partial(_mlstm_kernel, seq_len=seq_len)
    return pl.pallas_call(
        kern,
        grid=(n_seq,),
        in_specs=[
            pl.BlockSpec((seq_len, 768), lambda b: (blk0 + b, 0)),
            pl.BlockSpec((seq_len, ML_WIDTH), lambda b: (blk0 + b, 0)),
            pl.BlockSpec((nchunk, N_GATES, ML_CHUNK), lambda b: (blk0 + b, 0, 0)),
            pl.BlockSpec((1, 2, ML_HEADS, HEAD_DIM, HEAD_DIM), lambda b: (b, 0, 0, 0, 0)),
            pl.BlockSpec((1, 2 * ML_HEADS, HEAD_DIM), lambda b: (b, 0, 0)),
            pl.BlockSpec((1, 2 * ML_HEADS, 128), lambda b: (b, 0, 0)),
            pl.BlockSpec((1, ML_WIDTH), lambda b: (0, 0)),
        ],
        out_specs=[
            pl.BlockSpec((seq_len, ML_WIDTH), lambda b: (b, 0)),
            pl.BlockSpec((1, 2, ML_HEADS, HEAD_DIM, HEAD_DIM), lambda b: (b, 0, 0, 0, 0)),
            pl.BlockSpec((1, 2 * ML_HEADS, HEAD_DIM), lambda b: (b, 0, 0)),
            pl.BlockSpec((1, 2 * ML_HEADS, 128), lambda b: (b, 0, 0)),
        ],
        out_shape=(
            jax.ShapeDtypeStruct((n_seq * seq_len, ML_WIDTH), BF16),
            jax.ShapeDtypeStruct((n_seq, 2, ML_HEADS, HEAD_DIM, HEAD_DIM), F32),
            jax.ShapeDtypeStruct((n_seq, 2 * ML_HEADS, HEAD_DIM), F32),
            jax.ShapeDtypeStruct((n_seq, 2 * ML_HEADS, 128), F32),
        ),
        scratch_shapes=[pltpu.VMEM((seq_len, ML_WIDTH), F32), pltpu.VMEM((seq_len, ML_WIDTH), F32)],
        compiler_params=_cparams(("arbitrary",)),
        name="mlstm_len%d" % seq_len,
    )(qkvm, om, gates, c0, n0, m0, norm_w)


def _stack_heads(q_ref, g, rows):
    return jnp.concatenate(
        [q_ref[:, HEAD_DIM * (GQA_GROUP * g + j):HEAD_DIM * (GQA_GROUP * g + j + 1)] for j in range(GQA_GROUP)],
        axis=0)


def _sink_column(sink_ref, g, rows):
    rid = lax.broadcasted_iota(jnp.int32, (GQA_GROUP * rows, 1), 0) // rows
    col = jnp.full((GQA_GROUP * rows, 1), sink_ref[GQA_GROUP * g], F32)
    for j in range(1, GQA_GROUP):
        col = jnp.where(rid == j, sink_ref[GQA_GROUP * g + j], col)
    return col


def _attn_ctx_kernel(sink_ref, q_ref, k_ref, v_ref, o_ref):
    scale = HEAD_DIM ** -0.5
    for g in range(ATT_KV_HEADS):
        k = k_ref[:, HEAD_DIM * g:HEAD_DIM * (g + 1)]
        v = v_ref[:, HEAD_DIM * g:HEAD_DIM * (g + 1)]
        qs = _stack_heads(q_ref, g, SEQ)
        s = lax.dot_general(qs, k, NT_DIMS, preferred_element_type=F32) * scale
        sink = _sink_column(sink_ref, g, SEQ)
        m = jnp.maximum(jnp.max(s, axis=1, keepdims=True), sink)
        p = jnp.exp(s - m)
        den = jnp.sum(p, axis=1, keepdims=True) + jnp.exp(sink - m)
        o = jnp.dot(p.astype(BF16), v, preferred_element_type=F32) / den
        for j in range(GQA_GROUP):
            hq = GQA_GROUP * g + j
            o_ref[:, HEAD_DIM * hq:HEAD_DIM * (hq + 1)] = o[SEQ * j:SEQ * (j + 1)].astype(BF16)


def _attn_context(sink_l, qa, ka, va):
    return pl.pallas_call(
        _attn_ctx_kernel,
        grid=(BATCH,),
        in_specs=[
            pl.BlockSpec(memory_space=pltpu.SMEM),
            pl.BlockSpec((SEQ, ATT_WIDTH), lambda b: (b, 0)),
            pl.BlockSpec((SEQ, KV_WIDTH), lambda b: (b, 0)),
            pl.BlockSpec((SEQ, KV_WIDTH), lambda b: (b, 0)),
        ],
        out_specs=pl.BlockSpec((SEQ, ATT_WIDTH), lambda b: (b, 0)),
        out_shape=jax.ShapeDtypeStruct((N_CTX, ATT_WIDTH), BF16),
        compiler_params=_cparams(("arbitrary",)),
        name="attn_context",
    )(sink_l, qa, ka, va)


def _attn_lat_kernel(sink_ref, q_ref, k_ref, v_ref, ck_ref, cv_ref, o_ref):
    scale = HEAD_DIM ** -0.5
    nb = pl.program_id(1)
    win = 3 * ATT_BLOCK
    start = pl.multiple_of(jnp.clip((nb - 1) * ATT_BLOCK, 0, DEC_SEQ - win), ATT_BLOCK)
    kw = k_ref[pl.ds(start, win), :]
    vw = v_ref[pl.ds(start, win), :]
    ck = ck_ref[0].astype(BF16)
    cv = cv_ref[0].astype(BF16)
    rows = GQA_GROUP * ATT_BLOCK
    qpos = nb * ATT_BLOCK + lax.broadcasted_iota(jnp.int32, (rows, win), 0) % ATT_BLOCK
    kpos = start + lax.broadcasted_iota(jnp.int32, (rows, win), 1)
    valid = jnp.abs(qpos - kpos) <= WINDOW
    for g in range(ATT_KV_HEADS):
        sl = slice(HEAD_DIM * g, HEAD_DIM * (g + 1))
        qs = _stack_heads(q_ref, g, ATT_BLOCK)
        s_loc = lax.dot_general(qs, kw[:, sl], NT_DIMS, preferred_element_type=F32) * scale
        s_loc = jnp.where(valid, s_loc, NEG_BIG)
        s_ctx = lax.dot_general(qs, ck[:, sl], NT_DIMS, preferred_element_type=F32) * scale
        sink = _sink_column(sink_ref, g, ATT_BLOCK)
        m = jnp.maximum(jnp.maximum(jnp.max(s_loc, axis=1, keepdims=True),
                                    jnp.max(s_ctx, axis=1, keepdims=True)), sink)
        p_loc = jnp.exp(s_loc - m)
        p_ctx = jnp.exp(s_ctx - m)
        den = (jnp.sum(p_loc, axis=1, keepdims=True) + jnp.sum(p_ctx, axis=1, keepdims=True)
               + jnp.exp(sink - m))
        o = (jnp.dot(p_loc.astype(BF16), vw[:, sl], preferred_element_type=F32)
             + jnp.dot(p_ctx.astype(BF16), cv[:, sl], preferred_element_type=F32)) / den
        for j in range(GQA_GROUP):
            hq = GQA_GROUP * g + j
            o_ref[:, HEAD_DIM * hq:HEAD_DIM * (hq + 1)] = o[ATT_BLOCK * j:ATT_BLOCK * (j + 1)].astype(BF16)


def _attn_latent(sink_l, qa, ka, va, cache_k_l, cache_v_l):
    nblk = DEC_SEQ // ATT_BLOCK
    q0 = N_CTX // ATT_BLOCK
    s0 = N_CTX // DEC_SEQ
    return pl.pallas_call(
        _attn_lat_kernel,
        grid=(DEC_BATCH, nblk),
        in_specs=[
            pl.BlockSpec(memory_space=pltpu.SMEM),
            pl.BlockSpec((ATT_BLOCK, ATT_WIDTH), lambda b, n: (q0 + b * nblk + n, 0)),
            pl.BlockSpec((DEC_SEQ, KV_WIDTH), lambda b, n: (s0 + b, 0)),
            pl.BlockSpec((DEC_SEQ, KV_WIDTH), lambda b, n: (s0 + b, 0)),
            pl.BlockSpec((1, PAST_LEN, KV_WIDTH), lambda b, n: (b, 0, 0)),
            pl.BlockSpec((1, PAST_LEN, KV_WIDTH), lambda b, n: (b, 0, 0)),
        ],
        out_specs=pl.BlockSpec((ATT_BLOCK, ATT_WIDTH), lambda b, n: (b * nblk + n, 0)),
        out_shape=jax.ShapeDtypeStruct((N_LAT, ATT_WIDTH), BF16),
        compiler_params=_cparams(("arbitrary", "arbitrary")),
        name="attn_latent",
    )(sink_l, qa, ka, va, cache_k_l, cache_v_l)


def _s5_kernel(u_ref, bd_ref, cd_ref, are_ref, aim_ref, x0_ref, y_ref, xl_ref,
               bu_ref, xs_ref, st_ref, le_ref, ab_ref, *, n_seq, n_seg, steps):
    R = n_seq * n_seg
    tb = S5_TB_ROWS // R
    nblk = steps // tb
    W = S5_LANES
    n_tiles = R // 8

    for d in range(2):
        ab_ref[0] = jnp.broadcast_to(are_ref[d:d + 1, :], (8, W))
        ab_ref[1] = jnp.broadcast_to(aim_ref[d:d + 1, :], (8, W))

        def scan_block(blk, store):
            pos = blk if d == 0 else nblk - 1 - blk
            row0 = pl.multiple_of(pos * (tb * R), tb * R)
            ub = u_ref[pl.ds(row0, tb * R), :].astype(BF16)
            bu_ref[...] = jnp.dot(ub, bd_ref[d], preferred_element_type=F32)

            def step(jj, carry):
                jpos = jj if d == 0 else tb - 1 - jj
                base = pl.multiple_of(jpos * R, R)

                def tile(t, c2):
                    r = pl.multiple_of(t * 8, 8)
                    a_re = ab_ref[0]
                    a_im = ab_ref[1]
                    s_re = st_ref[pl.ds(r, 8), 0:W]
                    s_im = st_ref[pl.ds(r, 8), W:2 * W]
                    n_re = a_re * s_re - a_im * s_im + bu_ref[pl.ds(base + r, 8), 0:W]
                    n_im = a_re * s_im + a_im * s_re + bu_ref[pl.ds(base + r, 8), W:2 * W]
                    st_ref[pl.ds(r, 8), 0:W] = n_re
                    st_ref[pl.ds(r, 8), W:2 * W] = n_im
                    if store:
                        xs_ref[pl.ds(base + r, 8), 0:W] = n_re
                        xs_ref[pl.ds(base + r, 8), W:2 * W] = n_im
                    return c2

                lax.fori_loop(0, n_tiles, tile, 0)
                return carry

            lax.fori_loop(0, tb, step, 0)
            if store:
                yb = jnp.dot(xs_ref[...].astype(BF16), cd_ref[...], preferred_element_type=F32)
                if d == 0:
                    y_ref[pl.ds(row0, tb * R), :] = yb
                else:
                    y_ref[pl.ds(row0, tb * R), :] = y_ref[pl.ds(row0, tb * R), :] + yb

        if n_seg == 1:
            st_ref[...] = x0_ref[d]
            lax.fori_loop(0, nblk, lambda b, c: (scan_block(b, True), c)[1], 0)
            xl_ref[d] = st_ref[...]
        else:
            st_ref[...] = jnp.zeros((R, 2 * W), F32)
            lax.fori_loop(0, nblk, lambda b, c: (scan_block(b, False), c)[1], 0)
            le_ref[...] = st_ref[...]
            p_re = are_ref[d:d + 1, :]
            p_im = aim_ref[d:d + 1, :]
            for _ in range(int(math.log2(steps))):
                p_re, p_im = p_re * p_re - p_im * p_im, 2.0 * p_re * p_im
            first = 0 if d == 0 else n_seg - 1
            st_ref[pl.ds(first * n_seq, n_seq), :] = x0_ref[d]

            for i in range(n_seg - 1):
                src = i if d == 0 else n_seg - 1 - i
                dst = src + 1 if d == 0 else src - 1
                s_re = st_ref[src * n_seq:(src + 1) * n_seq, 0:W]
                s_im = st_ref[src * n_seq:(src + 1) * n_seq, W:2 * W]
                st_ref[dst * n_seq:(dst + 1) * n_seq, 0:W] = (
                    p_re * s_re - p_im * s_im + le_ref[src * n_seq:(src + 1) * n_seq, 0:W])
                st_ref[dst * n_seq:(dst + 1) * n_seq, W:2 * W] = (
                    p_re * s_im + p_im * s_re + le_ref[src * n_seq:(src + 1) * n_seq, W:2 * W])
            lax.fori_loop(0, nblk, lambda b, c: (scan_block(b, True), c)[1], 0)
            lastseg = n_seg - 1 if d == 0 else 0
            xl_ref[d] = st_ref[pl.ds(lastseg * n_seq, n_seq), :]


def _s5_scan(u_perm, bd, cd, a_re, a_im, x0, *, n_seq, n_seg, steps):
    R = n_seq * n_seg
    rows = steps * R
    kern = functools.partial(_s5_kernel, n_seq=n_seq, n_seg=n_seg, steps=steps)
    vm = pl.BlockSpec(memory_space=pltpu.VMEM)
    return pl.pallas_call(
        kern,
        in_specs=[vm] * 6,
        out_specs=[vm, vm],
        out_shape=(
            jax.ShapeDtypeStruct((rows, S5_WIDTH), F32),
            jax.ShapeDtypeStruct((2, n_seq, 2 * S5_LANES), F32),
        ),
        scratch_shapes=[
            pltpu.VMEM((S5_TB_ROWS, 2 * S5_LANES), F32),
            pltpu.VMEM((S5_TB_ROWS, 2 * S5_LANES), F32),
            pltpu.VMEM((R, 2 * S5_LANES), F32),
            pltpu.VMEM((R, 2 * S5_LANES), F32),
            pltpu.VMEM((2, 8, S5_LANES), F32),
        ],
        compiler_params=_cparams(None),
        name="s5_scan_seg%d" % n_seg,
    )(u_perm, bd, cd, a_re, a_im, x0)


def _s5_discretise(lam_re, lam_im, log_dt, b_re, b_im, c_re, c_im):
    dt = jnp.exp(log_dt)[..., None]
    er = jnp.exp(lam_re * dt)
    a_re = er * jnp.cos(lam_im * dt)
    a_im = er * jnp.sin(lam_im * dt)
    nr, ni = a_re - 1.0, a_im
    den = lam_re * lam_re + lam_im * lam_im
    cr = (nr * lam_re + ni * lam_im) / den
    ci = (ni * lam_re - nr * lam_im) / den
    bb_re = cr[..., None] * b_re[None] - ci[..., None] * b_im[None]
    bb_im = cr[..., None] * b_im[None] + ci[..., None] * b_re[None]
    eye = jnp.eye(S5_GROUPS, dtype=F32)
    bd_re = jnp.einsum('dgnq,gh->dgqhn', bb_re, eye).reshape(2, S5_WIDTH, S5_LANES)
    bd_im = jnp.einsum('dgnq,gh->dgqhn', bb_im, eye).reshape(2, S5_WIDTH, S5_LANES)
    bd = jnp.concatenate([bd_re, bd_im], axis=-1).astype(BF16)
    cd_re = jnp.einsum('gpn,gh->gnhp', c_re, eye).reshape(S5_LANES, S5_WIDTH)
    cd_im = jnp.einsum('gpn,gh->gnhp', c_im, eye).reshape(S5_LANES, S5_WIDTH)
    cd = jnp.concatenate([cd_re, -cd_im], axis=0).astype(BF16)
    return bd, cd, a_re.reshape(2, S5_LANES), a_im.reshape(2, S5_LANES)


def _post_kernel(*refs, moe):
    if moe:
        (x_ref, mod_ref, hm_ref, ha_ref, y_ref, u_ref, d_ref, gw_ref, gb_ref, wo_ref, nw_ref, rt_ref,
         x1_ref, h2_ref, gates_ref) = refs
    else:
        (x_ref, mod_ref, hm_ref, ha_ref, y_ref, u_ref, d_ref, gw_ref, gb_ref, wo_ref, nw_ref,
         x1_ref, h2_ref) = refs
    y = y_ref[...] + d_ref[...] * u_ref[...]
    z = 0.5 * y * (1.0 + lax.erf(y * (0.5 ** 0.5)))
    hs = z * jax.nn.sigmoid(jnp.dot(z.astype(BF16), gw_ref[...], preferred_element_type=F32) + gb_ref[...])
    mix = (jnp.dot(hm_ref[...], wo_ref[0:256, :], preferred_element_type=F32)
           + jnp.dot(ha_ref[...], wo_ref[256:768, :], preferred_element_type=F32)
           + jnp.dot(hs.astype(BF16), wo_ref[768:1024, :], preferred_element_type=F32))
    m = mod_ref[0]
    x1 = x_ref[...] + m[2:3] * mix
    x1_ref[...] = x1
    h2 = _rmsnorm(x1, nw_ref[...]) * (1.0 + m[4:5]) + m[3:4]
    h2_ref[...] = h2.astype(BF16)
    if moe:
        logits = jnp.dot(h2, rt_ref[...], precision=HI, preferred_element_type=F32)
        idx = lax.broadcasted_iota(jnp.int32, logits.shape, 1)
        v1 = jnp.max(logits, axis=1, keepdims=True)
        i1 = jnp.min(jnp.where(logits == v1, idx, N_EXPERTS), axis=1, keepdims=True)
        rest = jnp.where(idx == i1, -jnp.inf, logits)
        v2 = jnp.max(rest, axis=1, keepdims=True)
        i2 = jnp.min(jnp.where(rest == v2, idx, N_EXPERTS), axis=1, keepdims=True)
        e2 = jnp.exp(v2 - v1)
        w1 = 1.0 / (1.0 + e2)
        w2 = e2 / (1.0 + e2)
        gates_ref[...] = jnp.where(idx == i1, w1, 0.0) + jnp.where(idx == i2, w2, 0.0)


def _post_mix(x, mod_l, hm, ha, y, u, s5_d, glu_w, glu_b, w_out, norm_w, router=None):
    moe = router is not None
    row = lambda i: (i, 0)
    const = lambda i: (0, 0)
    in_specs = [
        pl.BlockSpec((TM, D_MODEL), row),
        pl.BlockSpec((1, 6, D_MODEL), lambda i: (_mod_row(i, TM), 0, 0)),
        pl.BlockSpec((TM, ML_WIDTH), row),
        pl.BlockSpec((TM, ATT_WIDTH), row),
        pl.BlockSpec((TM, S5_WIDTH), row),
        pl.BlockSpec((TM, S5_WIDTH), row),
        pl.BlockSpec((1, S5_WIDTH), const),
        pl.BlockSpec((S5_WIDTH, S5_WIDTH), const),
        pl.BlockSpec((1, S5_WIDTH), const),
        pl.BlockSpec((D_MODEL, D_MODEL), const),
        pl.BlockSpec((1, D_MODEL), const),
    ]
    args = [x, mod_l, hm, ha, y, u, s5_d, glu_w, glu_b, w_out, norm_w]
    out_specs = [pl.BlockSpec((TM, D_MODEL), row), pl.BlockSpec((TM, D_MODEL), row)]
    out_shape = [jax.ShapeDtypeStruct((N_TOK, D_MODEL), F32), jax.ShapeDtypeStruct((N_TOK, D_MODEL), BF16)]
    if moe:
        in_specs.append(pl.BlockSpec((D_MODEL, N_EXPERTS), const))
        args.append(router)
        out_specs.append(pl.BlockSpec((TM, N_EXPERTS), row))
        out_shape.append(jax.ShapeDtypeStruct((N_TOK, N_EXPERTS), F32))
    return pl.pallas_call(
        functools.partial(_post_kernel, moe=moe),
        grid=(N_TOK // TM,),
        in_specs=in_specs,
        out_specs=out_specs,
        out_shape=out_shape,
        compiler_params=_cparams(("arbitrary",)),
        name="post_mix_moe" if moe else "post_mix",
    )(*args)


def _ffn_kernel(*refs, moe):
    if moe:
        h_ref, x1_ref, mod_ref, gates_ref, wg_ref, wu_ref, wd_ref, o_ref, acc_ref = refs
        e = pl.program_id(1)
        kf = pl.program_id(2)
        first = jnp.logical_and(e == 0, kf == 0)
        last = jnp.logical_and(e == pl.num_programs(1) - 1, kf == pl.num_programs(2) - 1)
        wg, wu, wd = wg_ref[0], wu_ref[0], wd_ref[0]
    else:
        h_ref, x1_ref, mod_ref, wg_ref, wu_ref, wd_ref, o_ref, acc_ref = refs
        kf = pl.program_id(1)
        first = kf == 0
        last = kf == pl.num_programs(1) - 1
        wg, wu, wd = wg_ref[...], wu_ref[...], wd_ref[...]

    @pl.when(first)
    def _():
        acc_ref[...] = jnp.zeros(acc_ref.shape, F32)

    h = h_ref[...]
    a = jnp.dot(h, wg, preferred_element_type=F32)
    b = jnp.dot(h, wu, preferred_element_type=F32)
    act = a * jax.nn.sigmoid(a) * b
    if moe:
        gts = gates_ref[...]
        lane = lax.broadcasted_iota(jnp.int32, gts.shape, 1)
        act = act * jnp.sum(jnp.where(lane == e, gts, 0.0), axis=1, keepdims=True)
    acc_ref[...] += jnp.dot(act.astype(BF16), wd, preferred_element_type=F32)

    @pl.when(last)
    def _():
        o_ref[...] = x1_ref[...] + mod_ref[0][5:6] * acc_ref[...]


def _ffn_dense(h2, x1, mod_l, wg, wu, wd):
    nk = D_FF // TF
    return pl.pallas_call(
        functools.partial(_ffn_kernel, moe=False),
        grid=(N_TOK // TM_FFN, nk),
        in_specs=[
            pl.BlockSpec((TM_FFN, D_MODEL), lambda i, k: (i, 0)),
            pl.BlockSpec((TM_FFN, D_MODEL), lambda i, k: (i, 0)),
            pl.BlockSpec((1, 6, D_MODEL), lambda i, k: (_mod_row(i, TM_FFN), 0, 0)),
            pl.BlockSpec((D_MODEL, TF), lambda i, k: (0, k)),
            pl.BlockSpec((D_MODEL, TF), lambda i, k: (0, k)),
            pl.BlockSpec((TF, D_MODEL), lambda i, k: (k, 0)),
        ],
        out_specs=pl.BlockSpec((TM_FFN, D_MODEL), lambda i, k: (i, 0)),
        out_shape=jax.ShapeDtypeStruct((N_TOK, D_MODEL), F32),
        scratch_shapes=[pltpu.VMEM((TM_FFN, D_MODEL), F32)],
        compiler_params=_cparams(("arbitrary", "arbitrary")),
        name="ffn_dense",
    )(h2, x1, mod_l, wg, wu, wd)


def _ffn_moe(h2, x1, mod_l, gates, wg, wu, wd):
    nk = D_FF // TF
    return pl.pallas_call(
        functools.partial(_ffn_kernel, moe=True),
        grid=(N_TOK // TM_FFN, N_EXPERTS, nk),
        in_specs=[
            pl.BlockSpec((TM_FFN, D_MODEL), lambda i, e, k: (i, 0)),
            pl.BlockSpec((TM_FFN, D_MODEL), lambda i, e, k: (i, 0)),
            pl.BlockSpec((1, 6, D_MODEL), lambda i, e, k: (_mod_row(i, TM_FFN), 0, 0)),
            pl.BlockSpec((TM_FFN, N_EXPERTS), lambda i, e, k: (i, 0)),
            pl.BlockSpec((1, D_MODEL, TF), lambda i, e, k: (e, 0, k)),
            pl.BlockSpec((1, D_MODEL, TF), lambda i, e, k: (e, 0, k)),
            pl.BlockSpec((1, TF, D_MODEL), lambda i, e, k: (e, k, 0)),
        ],
        out_specs=pl.BlockSpec((TM_FFN, D_MODEL), lambda i, e, k: (i, 0)),
        out_shape=jax.ShapeDtypeStruct((N_TOK, D_MODEL), F32),
        scratch_shapes=[pltpu.VMEM((TM_FFN, D_MODEL), F32)],
        compiler_params=_cparams(("arbitrary", "arbitrary", "arbitrary")),
        name="ffn_moe",
    )(h2, x1, mod_l, gates, wg, wu, wd)


def _final_kernel(x_ref, w_ref, o_ref):
    o_ref[...] = _rmsnorm(x_ref[...], w_ref[...])


def _final_norm(x, w):
    return pl.pallas_call(
        _final_kernel,
        grid=(N_TOK // TM_FFN,),
        in_specs=[pl.BlockSpec((TM_FFN, D_MODEL), lambda i: (i, 0)), pl.BlockSpec((1, D_MODEL), lambda i: (0, 0))],
        out_specs=pl.BlockSpec((TM_FFN, D_MODEL), lambda i: (i, 0)),
        out_shape=jax.ShapeDtypeStruct((N_TOK, D_MODEL), F32),
        compiler_params=_cparams(("arbitrary",)),
        name="final_norm",
    )(x, w)


def _s5_permute(u, n_seq, n_seg, steps):
    return u.reshape(n_seq, n_seg, steps, S5_WIDTH).transpose(2, 1, 0, 3).reshape(n_seq * n_seg * steps, S5_WIDTH)


def _s5_unpermute(y, n_seq, n_seg, steps):
    return y.reshape(steps, n_seg, n_seq, S5_WIDTH).transpose(2, 1, 0, 3).reshape(n_seq * n_seg * steps, S5_WIDTH)


LAT_SEG = 64


def kernel(x_prompt, x_sample, c, cache_k, cache_v, state_mlstm_c, state_mlstm_n, state_mlstm_m, state_s5_re, state_s5_im, c_ctx, w_mod, b_mod, norm_mix_w, norm_ffn_w, norm_f_w, w_in, w_out, mlstm_gate_b, mlstm_norm_w, attn_sink, s5_lam_re, s5_lam_im, s5_log_dt, s5_b_re, s5_b_im, s5_c_re, s5_c_im, s5_d, s5_glu_w, s5_glu_b, ffn_w_gate, ffn_w_up, ffn_w_down, moe_router, moe_w_gate, moe_w_up, moe_w_down):
    x = jnp.concatenate([x_prompt.reshape(N_CTX, D_MODEL), x_sample.reshape(N_LAT, D_MODEL)], axis=0)
    cvecs = jnp.concatenate([c_ctx[None, :], c, jnp.zeros((N_MOD_ROWS - 1 - DEC_BATCH, D_MODEL), F32)], axis=0)
    mod = _modulation(cvecs, w_mod, b_mod).reshape(DEPTH, N_MOD_ROWS, 6, D_MODEL)
    cos_tab, sin_tab = _rope_tables()

    zeros_c = jnp.zeros((BATCH, 2, ML_HEADS, HEAD_DIM, HEAD_DIM), F32)
    zeros_n = jnp.zeros((BATCH, 2 * ML_HEADS, HEAD_DIM), F32)
    zeros_m = jnp.zeros((BATCH, 2 * ML_HEADS, 128), F32)
    zeros_s5 = jnp.zeros((2, BATCH, 2 * S5_LANES), F32)

    outs = [[] for _ in range(7)]
    lat_steps = DEC_SEQ // LAT_SEG
    for l in range(DEPTH):
        w_in_l = w_in[l]
        w_main = jnp.concatenate(
            [w_in_l[:, 0:256], w_in_l[:, 256:512] * (HEAD_DIM ** -0.5), w_in_l[:, 512:1024], w_in_l[:, 1040:2064]],
            axis=1).astype(BF16)
        w_gt = w_in_l[:, 1024:1040].T
        qkvm, om, gates, qa, ka, va, kv_raw, u = _project(
            x, mod[l], norm_mix_w[l][None, :], w_main, w_gt, mlstm_gate_b[l][:, None], cos_tab, sin_tab)

        nw_m = mlstm_norm_w[l][None, :]
        hm_c, mc, mn, mm = _mlstm(qkvm, om, gates, zeros_c, zeros_n, zeros_m, nw_m,
                                  seq_len=SEQ, n_seq=BATCH, blk0=0)
        m0 = jnp.broadcast_to(state_mlstm_m[:, l].reshape(DEC_BATCH, 2 * ML_HEADS, 1), (DEC_BATCH, 2 * ML_HEADS, 128))
        hm_l, _, _, _ = _mlstm(qkvm, om, gates, state_mlstm_c[:, l],
                               state_mlstm_n[:, l].reshape(DEC_BATCH, 2 * ML_HEADS, HEAD_DIM), m0, nw_m,
                               seq_len=DEC_SEQ, n_seq=DEC_BATCH, blk0=N_CTX // DEC_SEQ)
        hm = jnp.concatenate([hm_c, hm_l], axis=0)

        sink_l = attn_sink[l]
        ha_c = _attn_context(sink_l, qa, ka, va)
        ha_l = _attn_latent(sink_l, qa, ka, va,
                            cache_k[:, l].reshape(DEC_BATCH, PAST_LEN, KV_WIDTH),
                            cache_v[:, l].reshape(DEC_BATCH, PAST_LEN, KV_WIDTH))
        ha = jnp.concatenate([ha_c, ha_l], axis=0)

        bd, cd, a_re, a_im = _s5_discretise(s5_lam_re[l], s5_lam_im[l], s5_log_dt[l], s5_b_re[l], s5_b_im[l],
                                            s5_c_re[l], s5_c_im[l])
        y_c, xl_c = _s5_scan(_s5_permute(u[:N_CTX], BATCH, 1, SEQ), bd, cd, a_re, a_im, zeros_s5,
                             n_seq=BATCH, n_seg=1, steps=SEQ)
        x0_l = jnp.concatenate([state_s5_re[:, l].reshape(DEC_BATCH, 2, S5_LANES),
                                state_s5_im[:, l].reshape(DEC_BATCH, 2, S5_LANES)], axis=-1).transpose(1, 0, 2)
        y_l, _ = _s5_scan(_s5_permute(u[N_CTX:], DEC_BATCH, LAT_SEG, lat_steps), bd, cd, a_re, a_im, x0_l,
                          n_seq=DEC_BATCH, n_seg=LAT_SEG, steps=lat_steps)
        y = jnp.concatenate([_s5_unpermute(y_c, BATCH, 1, SEQ), _s5_unpermute(y_l, DEC_BATCH, LAT_SEG, lat_steps)],
                            axis=0)

        i = l // 2
        router = moe_router[i] if l % 2 == 1 else None
        post = _post_mix(x, mod[l], hm, ha, y, u, s5_d[l][None, :], s5_glu_w[l].astype(BF16), s5_glu_b[l][None, :],
                         w_out[l].astype(BF16), norm_ffn_w[l][None, :], router)
        if l % 2 == 0:
            x1, h2 = post
            x = _ffn_dense(h2, x1, mod[l], ffn_w_gate[i].astype(BF16), ffn_w_up[i].astype(BF16),
                           ffn_w_down[i].astype(BF16))
        else:
            x1, h2, gts = post
            x = _ffn_moe(h2, x1, mod[l], gts, moe_w_gate[i].astype(BF16), moe_w_up[i].astype(BF16),
                         moe_w_down[i].astype(BF16))

        kvc = kv_raw[:N_CTX].reshape(BATCH, SEQ, 2, ATT_KV_HEADS, HEAD_DIM)
        outs[0].append(kvc[:, :, 0])
        outs[1].append(kvc[:, :, 1])
        outs[2].append(mc)
        outs[3].append(mn.reshape(BATCH, 2, ML_HEADS, HEAD_DIM))
        outs[4].append(mm[:, :, 0].reshape(BATCH, 2, ML_HEADS))
        xl = xl_c.transpose(1, 0, 2)
        outs[5].append(xl[:, :, :S5_LANES].reshape(BATCH, 2, S5_GROUPS, S5_STATE))
        outs[6].append(xl[:, :, S5_LANES:].reshape(BATCH, 2, S5_GROUPS, S5_STATE))

    yfin = _final_norm(x, norm_f_w[None, :])
    y_prompt = yfin[:N_CTX].reshape(BATCH, SEQ, D_MODEL)
    y_sample = yfin[N_CTX:].reshape(DEC_BATCH, DEC_SEQ, D_MODEL)
    stacked = [jnp.stack(o, axis=1) for o in outs]
    return (y_prompt, y_sample) + tuple(stacked)
```
